```python
import jax, jax.numpy as jnp
from jax import lax
import numpy as np

D_MODEL = 1024
BATCH = 4
SEQ = 8192
DEPTH = 1

HGRN_HEADS = 8
HGRN_HEAD_DIM = 128
D_A = HGRN_HEADS * HGRN_HEAD_DIM
CHUNK = 64
D_B = D_MODEL
CONV_WIDTH = 31
D_FF = 4 * D_MODEL
EPS = 1e-6

SPLITS = (D_A, 2 * D_A, 3 * D_A, 4 * D_A, 4 * D_A + 2 * D_B, 4 * D_A + 2 * D_B + D_MODEL)
D_IN = 4 * D_A + 2 * D_B + 2 * D_MODEL

kernel_name = "hgrn2_conformer_gated_hybrid"


def rms_norm(x, g):
    xf = x.astype(jnp.float32)
    y = xf * lax.rsqrt(jnp.mean(xf * xf, axis=-1, keepdims=True) + EPS)
    return (y * g.astype(jnp.float32)).astype(x.dtype)


def layer_norm(x, g, b):
    xf = x.astype(jnp.float32)
    mu = jnp.mean(xf, axis=-1, keepdims=True)
    xc = xf - mu
    var = jnp.mean(xc * xc, axis=-1, keepdims=True)
    y = xc * lax.rsqrt(var + EPS) * g.astype(jnp.float32) + b.astype(jnp.float32)
    return y.astype(x.dtype)


def hgrn2_chunked(q, k, v, log_f):
    B, S, H, dk = q.shape
    dv = v.shape[-1]
    n = S // CHUNK

    def to_chunks(t):
        return t.reshape(B, n, CHUNK, H, t.shape[-1]).transpose(1, 0, 3, 2, 4)

    causal = jnp.tril(jnp.ones((CHUNK, CHUNK), dtype=bool))

    def step(state, inp):
        qc, kc, vc, lf = inp
        b = jnp.cumsum(lf, axis=2)
        o_inter = jnp.einsum('bhtd,bhdv->bhtv', qc * jnp.exp(b), state)
        diff = b[:, :, :, None, :] - b[:, :, None, :, :]
        decay = jnp.exp(jnp.where(causal[None, None, :, :, None], diff, -jnp.inf))
        scores = jnp.einsum('bhtd,bhsd,bhtsd->bhts', qc, kc, decay)
        o_intra = jnp.einsum('bhts,bhsv->bhtv', scores, vc)
        b_last = b[:, :, -1:, :]
        k_dec = kc * jnp.exp(b_last - b)
        new_state = (jnp.exp(b_last[:, :, 0, :])[..., None] * state
                     + jnp.einsum('bhsd,bhsv->bhdv', k_dec, vc))
        return new_state, o_inter + o_intra

    s0 = jnp.zeros((B, H, dk, dv), jnp.float32)
    _, o = lax.scan(step, s0, (to_chunks(q), to_chunks(k), to_chunks(v), to_chunks(log_f)))
    return o.transpose(1, 0, 3, 2, 4).reshape(B, S, H, dv)


def causal_depthwise_conv(u, w, b):
    y = lax.conv_general_dilated(
        u, w[:, None, :].astype(u.dtype), window_strides=(1,),
        padding=[(CONV_WIDTH - 1, 0)],
        dimension_numbers=('NWC', 'WIO', 'NWC'),
        feature_group_count=u.shape[-1])
    return y + b.astype(u.dtype)


def setup_inputs(seed: int = 0) -> dict:
    key = jax.random.key(seed)
    ks = jax.random.split(key, 16)
    f32 = jnp.float32

    def nrm(k, shape, scale):
        return jax.random.normal(k, shape, f32) * scale

    return {
        "x": nrm(ks[0], (BATCH, SEQ, D_MODEL), 1.0),
        "norm_mix_g": 1.0 + nrm(ks[1], (DEPTH, D_MODEL), 0.02),
        "w_in": nrm(ks[2], (DEPTH, D_MODEL, D_IN), D_MODEL ** -0.5),
        "lb_param": nrm(ks[3], (DEPTH + 1, D_A), 0.5),
        "hgrn_norm_g": 1.0 + nrm(ks[4], (DEPTH, HGRN_HEAD_DIM), 0.02),
        "w_a_out": nrm(ks[5], (DEPTH, D_A, D_MODEL), D_A ** -0.5),
        "conv_w": nrm(ks[6], (DEPTH, CONV_WIDTH, D_B), CONV_WIDTH ** -0.5),
        "conv_b": nrm(ks[7], (DEPTH, D_B), 0.02),
        "conv_ln_g": 1.0 + nrm(ks[8], (DEPTH, D_B), 0.02),
        "conv_ln_b": nrm(ks[9], (DEPTH, D_B), 0.02),
        "w_b_out": nrm(ks[10], (DEPTH, D_B, D_MODEL), D_B ** -0.5),
        "w_out": nrm(ks[11], (DEPTH, D_MODEL, D_MODEL), D_MODEL ** -0.5),
        "norm_mlp_g": 1.0 + nrm(ks[12], (DEPTH, D_MODEL), 0.02),
        "w_mlp_in": nrm(ks[13], (DEPTH, D_MODEL, D_FF), D_MODEL ** -0.5),
        "w_mlp_out": nrm(ks[14], (DEPTH, D_FF, D_MODEL), D_FF ** -0.5),
        "norm_final_g": 1.0 + nrm(ks[15], (D_MODEL,), 0.02),
    }


def reference(x, norm_mix_g, w_in, lb_param, hgrn_norm_g, w_a_out, conv_w, conv_b,
              conv_ln_g, conv_ln_b, w_b_out, w_out, norm_mlp_g, w_mlp_in, w_mlp_out,
              norm_final_g):
    B, S, _ = x.shape
    dt = x.dtype
    lb_all = jnp.cumsum(jax.nn.softmax(lb_param.astype(jnp.float32), axis=0), axis=0)

    for l in range(DEPTH):
        h = rms_norm(x, norm_mix_g[l])
        proj = jnp.einsum('bsd,de->bse', h, w_in[l])
        q, f_logit, i_in, g_out, glu_in, gate_a, gate_b = jnp.split(proj, SPLITS, axis=-1)

        lb = lb_all[l]
        zf = f_logit.astype(jnp.float32)
        log_f = jnp.log(lb + (1.0 - lb) * jax.nn.sigmoid(zf))
        k_in = (1.0 - lb) * jax.nn.sigmoid(-zf)
        hd = (B, S, HGRN_HEADS, HGRN_HEAD_DIM)
        o_a = hgrn2_chunked(jax.nn.silu(q.astype(jnp.float32)).reshape(hd),
                            k_in.reshape(hd), i_in.astype(jnp.float32).reshape(hd),
                            log_f.reshape(hd))
        o_a = rms_norm(o_a, hgrn_norm_g[l]).reshape(B, S, D_A).astype(dt)
        o_a = o_a * jax.nn.silu(g_out)
        branch_a = jnp.einsum('bse,ed->bsd', o_a, w_a_out[l])

        glu_a, glu_b = jnp.split(glu_in, 2, axis=-1)
        u = glu_a * jax.nn.sigmoid(glu_b)
        u = causal_depthwise_conv(u, conv_w[l], conv_b[l])
        u = jax.nn.silu(layer_norm(u, conv_ln_g[l], conv_ln_b[l]))
        branch_b = jnp.einsum('bsc,cd->bsd', u, w_b_out[l])

        merged = jax.nn.sigmoid(gate_a) * branch_a + jax.nn.sigmoid(gate_b) * branch_b
        x = x + jnp.einsum('bsd,de->bse', merged, w_out[l])

        h2 = rms_norm(x, norm_mlp_g[l])
        z = jax.nn.relu(jnp.einsum('bsd,df->bsf', h2, w_mlp_in[l]))
        x = x + jnp.einsum('bsf,fd->bsd', z * z, w_mlp_out[l])

    return rms_norm(x, norm_final_g)
```

```python
import functools

import jax
import jax.numpy as jnp
from jax import lax
from jax.experimental import pallas as pl
from jax.experimental.pallas import tpu as pltpu

F32 = jnp.float32
BF16 = jnp.bfloat16

EPS = 1e-6
HEAD_DIM = 128
CHUNK = 64
HALO = 32
CONV_ROWS = 32
CONV_LANES = 256
TM_MIX = 256
TM_MLP = 512
FF_TILE = 1024
VMEM_LIMIT = 56 * 1024 * 1024


def _rms(x, g):
    ms = jnp.mean(x * x, axis=-1, keepdims=True)
    return x * lax.rsqrt(ms + EPS) * g


def _sigmoid(z):
    return 1.0 / (1.0 + jnp.exp(-z))


def _dot(a, b):
    return jnp.dot(a, b, preferred_element_type=F32)


def _dot_nt(a, b):
    return lax.dot_general(a, b, (((1,), (1,)), ((), ())), preferred_element_type=F32)


def _dot_tn(a, b):
    return lax.dot_general(a, b, (((0,), (0,)), ((), ())), preferred_element_type=F32)


_LEVEL_H = (32, 16, 8, 4)


def _hgrn_kernel(x_ref, ng_ref, w_ref, lbp_ref, hg_ref, wa_ref, o_ref,
                 st_ref, q_sc, z_sc, v_sc, g_sc, oa_sc, *, layer, heads):
    tm = x_ref.shape[1]
    d_a = q_sc.shape[1]

    @pl.when(pl.program_id(1) == 0)
    def _():
        st_ref[...] = jnp.zeros_like(st_ref)

    hn = _rms(x_ref[0], ng_ref[...]).astype(BF16)
    q_sc[...] = _dot(hn, w_ref[:, 0 * d_a:1 * d_a])
    z_sc[...] = _dot(hn, w_ref[:, 1 * d_a:2 * d_a])
    v_sc[...] = _dot(hn, w_ref[:, 2 * d_a:3 * d_a]).astype(BF16)
    g_sc[...] = _dot(hn, w_ref[:, 3 * d_a:4 * d_a])

    p = lbp_ref[...]
    pe = jnp.exp(p - jnp.max(p, axis=0, keepdims=True))
    lb = (jnp.sum(pe[:layer + 1], axis=0, keepdims=True)
          / jnp.sum(pe, axis=0, keepdims=True))
    one_m_lb = 1.0 - lb
    hg = hg_ref[...]

    ti = lax.broadcasted_iota(jnp.int32, (CHUNK, CHUNK), 0)
    si = lax.broadcasted_iota(jnp.int32, (CHUNK, CHUNK), 1)
    ltri = (si <= ti).astype(BF16)
    x_ts = ti ^ si
    lev = jnp.where(si > ti, -1, 0)
    for j in range(6):
        lev = jnp.where((si < ti) & (x_ts >= (1 << j)), j + 1, lev)

    sub = lax.broadcasted_iota(jnp.int32, (8, d_a), 0)
    pos4 = sub & 3

    def chunk(c, carry):
        r0 = pl.multiple_of(c * CHUNK, CHUNK)
        rows = pl.ds(r0, CHUNK)
        qr = q_sc[rows, :]
        qs = qr * _sigmoid(qr)
        k = one_m_lb * (1.0 - _sigmoid(z_sc[rows, :]))
        f = 1.0 - k
        lf = jnp.log(f)
        lf_hi = lf.astype(BF16)
        lf_lo = (lf - lf_hi.astype(F32)).astype(BF16)
        b = _dot(ltri, lf_hi) + _dot(ltri, lf_lo)
        b_last = b[CHUNK - 1:CHUNK, :]
        qe = (qs * jnp.exp(b)).astype(BF16)
        kd = (k * jnp.exp(b_last - b)).astype(BF16)
        dec = jnp.exp(b_last)
        vb = v_sc[rows, :]
        gr = g_sc[rows, :]
        gate = gr * _sigmoid(gr)

        q_lv = [qs.astype(BF16)]
        k_lv = [k.astype(BF16)]
        q_lv.append((qs * f).astype(BF16))
        k_lv.append(k_lv[0])
        e2_parts = []
        for g8 in range(CHUNK // 8):
            fg = f[g8 * 8:(g8 + 1) * 8, :]
            up = pltpu.roll(fg, 7, axis=0)
            dn = pltpu.roll(fg, 1, axis=0)
            e2_parts.append(jnp.where(pos4 == 0, up,
                                      jnp.where(pos4 == 1, 1.0,
                                                jnp.where(pos4 == 2, fg, fg * dn))))
        e2 = jnp.concatenate(e2_parts, axis=0)
        q_lv.append((qs * e2).astype(BF16))
        k_lv.append((k * e2).astype(BF16))
        for h in reversed(_LEVEL_H):
            parts = []
            for blk in range(CHUNK // (2 * h)):
                row = blk * 2 * h + h - 1
                parts.append(jnp.broadcast_to(b[row:row + 1, :], (2 * h, d_a)))
            r = parts[0] if len(parts) == 1 else jnp.concatenate(parts, axis=0)
            e = jnp.exp(-jnp.abs(b - r))
            q_lv.append((qs * e).astype(BF16))
            k_lv.append((k * e).astype(BF16))

        for hd in range(heads):
            sl = slice(hd * HEAD_DIM, (hd + 1) * HEAD_DIM)
            scores = jnp.zeros((CHUNK, CHUNK), F32)
            for li in range(len(q_lv)):
                s_l = _dot_nt(q_lv[li][:, sl], k_lv[li][:, sl])
                scores = jnp.where(lev == li, s_l, scores)
            st = st_ref[hd]
            o = (_dot_nt(qe[:, sl], st.astype(BF16))
                 + _dot(scores.astype(BF16), vb[:, sl]))
            ms = jnp.mean(o * o, axis=-1, keepdims=True)
            on = o * lax.rsqrt(ms + EPS) * hg
            oa_sc[rows, sl] = (on * gate[:, sl]).astype(BF16)
            st_ref[hd] = st * dec[:, sl] + _dot_tn(vb[:, sl], kd[:, sl])
        return carry

    lax.fori_loop(0, tm // CHUNK, chunk, 0)
    o_ref[0] = _dot(oa_sc[...], wa_ref[...]).astype(BF16)


def _conv_kernel(x_ref, ng_ref, wglu_ref, wgate_ref, cw_ref, cb_ref, lng_ref, lnb_ref,
                 wb_ref, wo_ref, ba_ref, o_ref, u_sc, y_sc, *, width):
    tm = x_ref.shape[1]
    d_b = y_sc.shape[1]

    @pl.when(pl.program_id(1) == 0)
    def _():
        u_sc[0:HALO, :] = jnp.zeros((HALO, d_b), F32)

    x = x_ref[0]
    hn = _rms(x, ng_ref[...]).astype(BF16)
    glu = _dot(hn, wglu_ref[...])
    u_sc[HALO:HALO + tm, :] = glu[:, :d_b] * _sigmoid(glu[:, d_b:])

    base = HALO - (width - 1)
    for rt in range(tm // CONV_ROWS):
        for lt in range(d_b // CONV_LANES):
            lanes = slice(lt * CONV_LANES, (lt + 1) * CONV_LANES)
            acc = jnp.zeros((CONV_ROWS, CONV_LANES), F32)
            for k in range(width):
                acc = acc + (u_sc[pl.ds(rt * CONV_ROWS + base + k, CONV_ROWS), lanes]
                             * cw_ref[k:k + 1, lanes])
            y_sc[rt * CONV_ROWS:(rt + 1) * CONV_ROWS, lanes] = acc
    u_sc[0:HALO, :] = u_sc[tm:tm + HALO, :]

    y = y_sc[...] + cb_ref[...]
    mu = jnp.mean(y, axis=-1, keepdims=True)
    yc = y - mu
    var = jnp.mean(yc * yc, axis=-1, keepdims=True)
    ln = yc * lax.rsqrt(var + EPS) * lng_ref[...] + lnb_ref[...]
    act = (ln * _sigmoid(ln)).astype(BF16)
    branch_b = _dot(act, wb_ref[...])

    gates = _sigmoid(_dot(hn, wgate_ref[...]))
    d_m = branch_b.shape[1]
    merged = gates[:, :d_m] * ba_ref[0].astype(F32) + gates[:, d_m:] * branch_b
    o_ref[0] = x + _dot(merged.astype(BF16), wo_ref[...])


def _mlp_kernel(x_ref, ng_ref, w1_ref, w2_ref, fg_ref, o_ref):
    x = x_ref[...]
    hn = _rms(x, ng_ref[...]).astype(BF16)
    acc = x
    for j in range(w1_ref.shape[1] // FF_TILE):
        cols = slice(j * FF_TILE, (j + 1) * FF_TILE)
        z = jnp.maximum(_dot(hn, w1_ref[:, cols]), 0.0)
        acc = acc + _dot((z * z).astype(BF16), w2_ref[cols, :])
    o_ref[...] = _rms(acc, fg_ref[...])


def _const_spec(shape):
    return pl.BlockSpec(shape, lambda *_: (0,) * len(shape), pipeline_mode=pl.Buffered(1))


def _layer(x, l, norm_mix_g, w_in, lb_param, hgrn_norm_g, w_a_out, conv_w, conv_b, conv_ln_g,
           conv_ln_b, w_b_out, w_out, norm_mlp_g, w_mlp_in, w_mlp_out, out_g):
    bsz, seq, d = x.shape
    d_a = w_a_out.shape[1]
    d_b = w_b_out.shape[1]
    heads = d_a // HEAD_DIM
    width = conv_w.shape[1]
    d_ff = w_mlp_in.shape[2]
    assert seq % TM_MIX == 0 and TM_MIX % CHUNK == 0 and (bsz * seq) % TM_MLP == 0
    assert width - 1 <= HALO <= TM_MIX and d_ff % FF_TILE == 0

    w_in_l = w_in[l].astype(BF16)
    w_qfig = w_in_l[:, :4 * d_a]
    w_glu = w_in_l[:, 4 * d_a:4 * d_a + 2 * d_b]
    w_gate = w_in_l[:, 4 * d_a + 2 * d_b:]
    row = lambda v: v.reshape(1, -1)
    mix_params = pltpu.CompilerParams(
        dimension_semantics=("arbitrary", "arbitrary"), vmem_limit_bytes=VMEM_LIMIT)
    tok_spec = pl.BlockSpec((1, TM_MIX, d), lambda b, s: (b, s, 0))
    grid = (bsz, seq // TM_MIX)

    branch_a = pl.pallas_call(
        functools.partial(_hgrn_kernel, layer=l, heads=heads),
        grid=grid,
        in_specs=[tok_spec, _const_spec((1, d)), _const_spec((d, 4 * d_a)),
                  _const_spec(lb_param.shape), _const_spec((1, HEAD_DIM)),
                  _const_spec((d_a, d))],
        out_specs=tok_spec,
        out_shape=jax.ShapeDtypeStruct((bsz, seq, d), BF16),
        scratch_shapes=[pltpu.VMEM((heads, HEAD_DIM, HEAD_DIM), F32),
                        pltpu.VMEM((TM_MIX, d_a), F32), pltpu.VMEM((TM_MIX, d_a), F32),
                        pltpu.VMEM((TM_MIX, d_a), BF16), pltpu.VMEM((TM_MIX, d_a), F32),
                        pltpu.VMEM((TM_MIX, d_a), BF16)],
        compiler_params=mix_params,
        name="hgrn_branch",
    )(x, row(norm_mix_g[l]), w_qfig, lb_param, row(hgrn_norm_g[l]), w_a_out[l].astype(BF16))

    x1 = pl.pallas_call(
        functools.partial(_conv_kernel, width=width),
        grid=grid,
        in_specs=[tok_spec, _const_spec((1, d)), _const_spec((d, 2 * d_b)),
                  _const_spec((d, 2 * d)), _const_spec((width, d_b)), _const_spec((1, d_b)),
                  _const_spec((1, d_b)), _const_spec((1, d_b)), _const_spec((d_b, d)),
                  _const_spec((d, d)), tok_spec],
        out_specs=tok_spec,
        out_shape=jax.ShapeDtypeStruct((bsz, seq, d), F32),
        scratch_shapes=[pltpu.VMEM((HALO + TM_MIX, d_b), F32), pltpu.VMEM((TM_MIX, d_b), F32)],
        compiler_params=mix_params,
        name="conv_merge",
    )(x, row(norm_mix_g[l]), w_glu, w_gate, conv_w[l], row(conv_b[l]), row(conv_ln_g[l]),
      row(conv_ln_b[l]), w_b_out[l].astype(BF16), w_out[l].astype(BF16), branch_a)

    n_tok = bsz * seq
    flat_spec = pl.BlockSpec((TM_MLP, d), lambda i: (i, 0))
    out = pl.pallas_call(
        _mlp_kernel,
        grid=(n_tok // TM_MLP,),
        in_specs=[flat_spec, _const_spec((1, d)), _const_spec((d, d_ff)),
                  _const_spec((d_ff, d)), _const_spec((1, d))],
        out_specs=flat_spec,
        out_shape=jax.ShapeDtypeStruct((n_tok, d), F32),
        compiler_params=pltpu.CompilerParams(
            dimension_semantics=("arbitrary",), vmem_limit_bytes=VMEM_LIMIT),
        name="mlp",
    )(x1.reshape(n_tok, d), row(norm_mlp_g[l]), w_mlp_in[l].astype(BF16),
      w_mlp_out[l].astype(BF16), row(out_g))
    return out.reshape(bsz, seq, d)


def kernel(x, norm_mix_g, w_in, lb_param, hgrn_norm_g, w_a_out, conv_w, conv_b, conv_ln_g,
           conv_ln_b, w_b_out, w_out, norm_mlp_g, w_mlp_in, w_mlp_out, norm_final_g):
    depth = w_in.shape[0]
    assert depth == 1, "final norm is fused into the (single) layer's MLP kernel"
    return _layer(x, 0, norm_mix_g, w_in, lb_param, hgrn_norm_g, w_a_out, conv_w, conv_b,
                  conv_ln_g, conv_ln_b, w_b_out, w_out, norm_mlp_g, w_mlp_in, w_mlp_out,
                  norm_final_g)
```

```python
import functools

import jax
import jax.numpy as jnp
from jax import lax
from jax.experimental import pallas as pl
from jax.experimental.pallas import tpu as pltpu

F32 = jnp.float32
BF16 = jnp.bfloat16

EPS = 1e-6
LOG2E = 1.4426950408889634
SUBLANES = 8
HEAD_DIM = 128
CHUNK = 64
SUB = 16
SUB_MIN_GATE = 2.0 ** -7
HALO = 32
CONV_ROWS = 64
CONV_LANES = 128
TM_MIX = 256
TM_MLP = 512
FF_TILE = 1024
VMEM_LIMIT = 56 * 1024 * 1024


def _rms(x, g):
    ms = jnp.mean(x * x, axis=-1, keepdims=True)
    return x * lax.rsqrt(ms + EPS) * g


def _sigmoid(z):
    return 0.5 * jnp.tanh(0.5 * z) + 0.5


def _silu(z):
    hz = 0.5 * z
    return hz * jnp.tanh(hz) + hz


def _dot(a, b):
    return jnp.dot(a, b, preferred_element_type=F32)


def _dot_nt(a, b):
    return lax.dot_general(a, b, (((1,), (1,)), ((), ())), preferred_element_type=F32)


def _dot_tn(a, b):
    return lax.dot_general(a, b, (((0,), (0,)), ((), ())), preferred_element_type=F32)


_LEVEL_H = (32, 16, 8, 4)


def _hgrn_kernel(x_ref, ng_ref, w_ref, lbp_ref, hg_ref, wa_ref, o_ref,
                 st_ref, q_sc, z_sc, v_sc, g_sc, oa_sc, *, layer, heads):
    tm = x_ref.shape[1]
    d_a = q_sc.shape[1]

    @pl.when(pl.program_id(1) == 0)
    def _():
        st_ref[...] = jnp.zeros_like(st_ref)

    hn = _rms(x_ref[0], ng_ref[...]).astype(BF16)
    q_sc[...] = _dot(hn, w_ref[:, 0 * d_a:1 * d_a])
    z_sc[...] = _dot(hn, w_ref[:, 1 * d_a:2 * d_a])
    v_sc[...] = _dot(hn, w_ref[:, 2 * d_a:3 * d_a]).astype(BF16)
    g_sc[...] = _dot(hn, w_ref[:, 3 * d_a:4 * d_a])

    p = lbp_ref[...]
    pe = jnp.exp(p - jnp.max(p, axis=0, keepdims=True))
    lb = (jnp.sum(pe[:layer + 1], axis=0, keepdims=True)
          / jnp.sum(pe, axis=0, keepdims=True))
    one_m_lb = 1.0 - lb
    hg = hg_ref[...]

    ti = lax.broadcasted_iota(jnp.int32, (CHUNK, CHUNK), 0)
    si = lax.broadcasted_iota(jnp.int32, (CHUNK, CHUNK), 1)
    ltri = (si <= ti).astype(BF16)
    x_ts = ti ^ si
    lev = jnp.where(si > ti, -1, 0)
    for j in range(6):
        lev = jnp.where((si < ti) & (x_ts >= (1 << j)), j + 1, lev)

    pos4 = lax.broadcasted_iota(jnp.int32, (SUBLANES, d_a), 0) & 3
    tri_sub = (lax.broadcasted_iota(jnp.int32, (SUB, SUB), 1)
               <= lax.broadcasted_iota(jnp.int32, (SUB, SUB), 0))

    def intra_any_gate(qs, k, f, b, qe, vb):
        q_lv = [qs.astype(BF16)]
        k_lv = [k.astype(BF16)]
        q_lv.append((qs * f).astype(BF16))
        k_lv.append(k_lv[0])
        e2_parts = []
        for g8 in range(CHUNK // SUBLANES):
            fg = f[g8 * SUBLANES:(g8 + 1) * SUBLANES, :]
            up = pltpu.roll(fg, SUBLANES - 1, axis=0)
            dn = pltpu.roll(fg, 1, axis=0)
            e2_parts.append(jnp.where(pos4 == 0, up,
                                      jnp.where(pos4 == 1, 1.0,
                                                jnp.where(pos4 == 2, fg, fg * dn))))
        e2 = jnp.concatenate(e2_parts, axis=0)
        q_lv.append((qs * e2).astype(BF16))
        k_lv.append((k * e2).astype(BF16))
        for h in reversed(_LEVEL_H):
            parts = []
            for blk in range(CHUNK // (2 * h)):
                row = blk * 2 * h + h - 1
                parts.append(jnp.broadcast_to(b[row:row + 1, :], (2 * h, d_a)))
            r = parts[0] if len(parts) == 1 else jnp.concatenate(parts, axis=0)
            e = jnp.exp2(-jnp.abs(b - r))
            q_lv.append((qs * e).astype(BF16))
            k_lv.append((k * e).astype(BF16))

        outs = []
        for hd in range(heads):
            sl = slice(hd * HEAD_DIM, (hd + 1) * HEAD_DIM)
            scores = jnp.zeros((CHUNK, CHUNK), F32)
            for li in range(len(q_lv)):
                s_l = _dot_nt(q_lv[li][:, sl], k_lv[li][:, sl])
                scores = jnp.where(lev == li, s_l, scores)
            outs.append(_dot(scores.astype(BF16), vb[:, sl]))
        return outs

    def intra_bounded_gate(qs, k, f, b, qe, vb):
        n_sub = CHUNK // SUB
        r_rows = [jnp.zeros((1, d_a), F32)] + [b[j * SUB - 1:j * SUB, :] for j in range(1, n_sub)]
        r_full = jnp.concatenate([jnp.broadcast_to(r, (SUB, d_a)) for r in r_rows], axis=0)
        kt = (k * jnp.exp2(r_full - b)).astype(BF16)
        q_sub = [qe] + [(qs[j * SUB:, :] * jnp.exp2(b[j * SUB:, :] - r_rows[j])).astype(BF16)
                        for j in range(1, n_sub)]
        outs = []
        for hd in range(heads):
            sl = slice(hd * HEAD_DIM, (hd + 1) * HEAD_DIM)
            blocks = [None] * n_sub
            for j in range(n_sub):
                js = slice(j * SUB, (j + 1) * SUB)
                s_j = _dot_nt(q_sub[j][:, sl], kt[js, sl])
                diag = jnp.where(tri_sub, s_j[:SUB, :], 0.0)
                s_j = diag if j == n_sub - 1 else jnp.concatenate([diag, s_j[SUB:, :]], axis=0)
                p_j = _dot(s_j.astype(BF16), vb[js, sl])
                for i in range(j, n_sub):
                    part = p_j[(i - j) * SUB:(i - j + 1) * SUB, :]
                    blocks[i] = part if blocks[i] is None else blocks[i] + part
            outs.append(jnp.concatenate(blocks, axis=0))
        return outs

    def make_chunk(intra):
        def chunk(c, carry):
            rows = pl.ds(pl.multiple_of(c * CHUNK, CHUNK), CHUNK)
            qs = _silu(q_sc[rows, :])
            k = one_m_lb * (1.0 - _sigmoid(z_sc[rows, :]))
            f = 1.0 - k
            vb = v_sc[rows, :]
            gate = _silu(g_sc[rows, :])
            lf = jnp.log(f) * LOG2E
            lf_hi = lf.astype(BF16)
            lf_lo = (lf - lf_hi.astype(F32)).astype(BF16)
            b = _dot(ltri, lf_hi) + _dot(ltri, lf_lo)
            b_last = b[CHUNK - 1:CHUNK, :]
            qe = (qs * jnp.exp2(b)).astype(BF16)
            kd = (k * jnp.exp2(b_last - b)).astype(BF16)
            dec = jnp.exp2(b_last)
            o_intra = intra(qs, k, f, b, qe, vb)
            for hd in range(heads):
                sl = slice(hd * HEAD_DIM, (hd + 1) * HEAD_DIM)
                st = st_ref[hd]
                o = _dot_nt(qe[:, sl], st.astype(BF16)) + o_intra[hd]
                ms = jnp.mean(o * o, axis=-1, keepdims=True)
                on = o * lax.rsqrt(ms + EPS) * hg
                oa_sc[rows, sl] = (on * gate[:, sl]).astype(BF16)
                st_ref[hd] = st * dec[:, sl] + _dot_tn(vb[:, sl], kd[:, sl])
            return carry
        return chunk

    gates_bounded = jnp.min(lb) >= SUB_MIN_GATE

    @pl.when(gates_bounded)
    def _():
        lax.fori_loop(0, tm // CHUNK, make_chunk(intra_bounded_gate), 0)

    @pl.when(jnp.logical_not(gates_bounded))
    def _():
        lax.fori_loop(0, tm // CHUNK, make_chunk(intra_any_gate), 0)

    o_ref[0] = _dot(oa_sc[...], wa_ref[...]).astype(BF16)


def _conv_kernel(x_ref, ng_ref, wglu_ref, wgate_ref, cw_ref, cb_ref, lng_ref, lnb_ref,
                 wb_ref, wo_ref, ba_ref, o_ref, u_sc, y_sc, *, width):
    tm = x_ref.shape[1]
    d_b = y_sc.shape[1]

    @pl.when(pl.program_id(1) == 0)
    def _():
        u_sc[0:HALO, :] = jnp.zeros((HALO, d_b), F32)

    x = x_ref[0]
    hn = _rms(x, ng_ref[...]).astype(BF16)
    glu = _dot(hn, wglu_ref[...])
    u_sc[HALO:HALO + tm, :] = glu[:, :d_b] * _sigmoid(glu[:, d_b:])

    for rt in range(tm // CONV_ROWS):
        for lt in range(d_b // CONV_LANES):
            lanes = slice(lt * CONV_LANES, (lt + 1) * CONV_LANES)
            win = u_sc[rt * CONV_ROWS:rt * CONV_ROWS + CONV_ROWS + HALO, lanes]
            acc = jnp.zeros((CONV_ROWS, CONV_LANES), F32)
            for r in range(SUBLANES):
                win_r = win if r == 0 else pltpu.roll(win, r, axis=0)
                for a in range((width - 1 - r) // SUBLANES + 1):
                    j = SUBLANES * a + r
                    start = HALO - SUBLANES * a
                    acc = acc + (win_r[start:start + CONV_ROWS, :]
                                 * cw_ref[width - 1 - j:width - j, lanes])
            y_sc[rt * CONV_ROWS:(rt + 1) * CONV_ROWS, lanes] = acc
    u_sc[0:HALO, :] = u_sc[tm:tm + HALO, :]

    y = y_sc[...] + cb_ref[...]
    mu = jnp.mean(y, axis=-1, keepdims=True)
    yc = y - mu
    var = jnp.mean(yc * yc, axis=-1, keepdims=True)
    ln = yc * lax.rsqrt(var + EPS) * lng_ref[...] + lnb_ref[...]
    branch_b = _dot(_silu(ln).astype(BF16), wb_ref[...])

    gates = _sigmoid(_dot(hn, wgate_ref[...]))
    d_m = branch_b.shape[1]
    merged = gates[:, :d_m] * ba_ref[0].astype(F32) + gates[:, d_m:] * branch_b
    o_ref[0] = x + _dot(merged.astype(BF16), wo_ref[...])


def _mlp_kernel(x_ref, ng_ref, w1_ref, w2_ref, fg_ref, o_ref):
    x = x_ref[...]
    hn = _rms(x, ng_ref[...]).astype(BF16)
    acc = x
    for j in range(w1_ref.shape[1] // FF_TILE):
        cols = slice(j * FF_TILE, (j + 1) * FF_TILE)
        z = jnp.maximum(_dot(hn, w1_ref[:, cols]), 0.0)
        acc = acc + _dot((z * z).astype(BF16), w2_ref[cols, :])
    o_ref[...] = _rms(acc, fg_ref[...])


def _const_spec(shape):
    return pl.BlockSpec(shape, lambda *_: (0,) * len(shape), pipeline_mode=pl.Buffered(1))


def _layer(x, l, norm_mix_g, w_in, lb_param, hgrn_norm_g, w_a_out, conv_w, conv_b, conv_ln_g,
           conv_ln_b, w_b_out, w_out, norm_mlp_g, w_mlp_in, w_mlp_out, out_g):
    bsz, seq, d = x.shape
    d_a = w_a_out.shape[1]
    d_b = w_b_out.shape[1]
    heads = d_a // HEAD_DIM
    width = conv_w.shape[1]
    d_ff = w_mlp_in.shape[2]
    assert seq % TM_MIX == 0 and TM_MIX % CHUNK == 0 and (bsz * seq) % TM_MLP == 0
    assert width - 1 <= HALO <= TM_MIX and d_ff % FF_TILE == 0
    assert TM_MIX % CONV_ROWS == 0 and d_b % CONV_LANES == 0

    w_in_l = w_in[l].astype(BF16)
    w_qfig = w_in_l[:, :4 * d_a]
    w_glu = w_in_l[:, 4 * d_a:4 * d_a + 2 * d_b]
    w_gate = w_in_l[:, 4 * d_a + 2 * d_b:]
    row = lambda v: v.reshape(1, -1)
    mix_params = pltpu.CompilerParams(
        dimension_semantics=("arbitrary", "arbitrary"), vmem_limit_bytes=VMEM_LIMIT)
    tok_spec = pl.BlockSpec((1, TM_MIX, d), lambda b, s: (b, s, 0))
    grid = (bsz, seq // TM_MIX)

    branch_a = pl.pallas_call(
        functools.partial(_hgrn_kernel, layer=l, heads=heads),
        grid=grid,
        in_specs=[tok_spec, _const_spec((1, d)), _const_spec((d, 4 * d_a)),
                  _const_spec(lb_param.shape), _const_spec((1, HEAD_DIM)),
                  _const_spec((d_a, d))],
        out_specs=tok_spec,
        out_shape=jax.ShapeDtypeStruct((bsz, seq, d), BF16),
        scratch_shapes=[pltpu.VMEM((heads, HEAD_DIM, HEAD_DIM), F32),
                        pltpu.VMEM((TM_MIX, d_a), F32), pltpu.VMEM((TM_MIX, d_a), F32),
                        pltpu.VMEM((TM_MIX, d_a), BF16), pltpu.VMEM((TM_MIX, d_a), F32),
                        pltpu.VMEM((TM_MIX, d_a), BF16)],
        compiler_params=mix_params,
        name="hgrn_branch",
    )(x, row(norm_mix_g[l]), w_qfig, lb_param, row(hgrn_norm_g[l]), w_a_out[l].astype(BF16))

    x1 = pl.pallas_call(
        functools.partial(_conv_kernel, width=width),
        grid=grid,
        in_specs=[tok_spec, _const_spec((1, d)), _const_spec((d, 2 * d_b)),
                  _const_spec((d, 2 * d)), _const_spec((width, d_b)), _const_spec((1, d_b)),
                  _const_spec((1, d_b)), _const_spec((1, d_b)), _const_spec((d_b, d)),
                  _const_spec((d, d)), tok_spec],
        out_specs=tok_spec,
        out_shape=jax.ShapeDtypeStruct((bsz, seq, d), F32),
        scratch_shapes=[pltpu.VMEM((HALO + TM_MIX, d_b), F32), pltpu.VMEM((TM_MIX, d_b), F32)],
        compiler_params=mix_params,
        name="conv_merge",
    )(x, row(norm_mix_g[l]), w_glu, w_gate, conv_w[l], row(conv_b[l]), row(conv_ln_g[l]),
      row(conv_ln_b[l]), w_b_out[l].astype(BF16), w_out[l].astype(BF16), branch_a)

    n_tok = bsz * seq
    flat_spec = pl.BlockSpec((TM_MLP, d), lambda i: (i, 0))
    out = pl.pallas_call(
        _mlp_kernel,
        grid=(n_tok // TM_MLP,),
        in_specs=[flat_spec, _const_spec((1, d)), _const_spec((d, d_ff)),
                  _const_spec((d_ff, d)), _const_spec((1, d))],
        out_specs=flat_spec,
        out_shape=jax.ShapeDtypeStruct((n_tok, d), F32),
        compiler_params=pltpu.CompilerParams(
            dimension_semantics=("arbitrary",), vmem_limit_bytes=VMEM_LIMIT),
        name="mlp",
    )(x1.reshape(n_tok, d), row(norm_mlp_g[l]), w_mlp_in[l].astype(BF16),
      w_mlp_out[l].astype(BF16), row(out_g))
    return out.reshape(bsz, seq, d)


def kernel(x, norm_mix_g, w_in, lb_param, hgrn_norm_g, w_a_out, conv_w, conv_b, conv_ln_g,
           conv_ln_b, w_b_out, w_out, norm_mlp_g, w_mlp_in, w_mlp_out, norm_final_g):
    depth = w_in.shape[0]
    assert depth == 1, "final norm is fused into the (single) layer's MLP kernel"
    return _layer(x, 0, norm_mix_g, w_in, lb_param, hgrn_norm_g, w_a_out, conv_w, conv_b,
                  conv_ln_g, conv_ln_b, w_b_out, w_out, norm_mlp_g, w_mlp_in, w_mlp_out,
                  norm_final_g)
```

```python
import functools

import jax
import jax.numpy as jnp
from jax import lax
from jax.experimental import pallas as pl
from jax.experimental.pallas import tpu as pltpu

F32 = jnp.float32
BF16 = jnp.bfloat16

EPS = 1e-6
LOG2E = 1.4426950408889634
SUBLANES = 8
HEAD_DIM = 128
CHUNK = 64
SUB = 16
SUB_MIN_GATE = 2.0 ** -7
HALO = 32
CONV_ROWS = 64
CONV_LANES = 128
TM_MIX = 512
TM_MLP = 512
FF_TILE = 1024
VMEM_LIMIT = 56 * 1024 * 1024


def _rms(x, g):
    ms = jnp.mean(x * x, axis=-1, keepdims=True)
    return x * lax.rsqrt(ms + EPS) * g


def _sigmoid(z):
    return 0.5 * jnp.tanh(0.5 * z) + 0.5


def _silu(z):
    hz = 0.5 * z
    return hz * jnp.tanh(hz) + hz


def _dot(a, b):
    return jnp.dot(a, b, preferred_element_type=F32)


def _dot_nt(a, b):
    return lax.dot_general(a, b, (((1,), (1,)), ((), ())), preferred_element_type=F32)


def _dot_tn(a, b):
    return lax.dot_general(a, b, (((0,), (0,)), ((), ())), preferred_element_type=F32)


_LEVEL_H = (32, 16, 8, 4)


def _mixer_kernel(x_ref, ng_ref, w_ref, lbp_ref, hg_ref, wa_ref, cw_ref, cb_ref, lng_ref,
                  lnb_ref, wb_ref, wo_ref, o_ref,
                  st_ref, q_sc, z_sc, v_sc, g_sc, oa_sc, u_sc, y_sc, gate_sc,
                  *, layer, heads, width):
    tm = x_ref.shape[1]
    d_a = q_sc.shape[1]
    d_b = y_sc.shape[1]
    d_m = o_ref.shape[2]
    assert tm // CHUNK == tm // CONV_ROWS

    @pl.when(pl.program_id(1) == 0)
    def _():
        st_ref[...] = jnp.zeros_like(st_ref)
        u_sc[0:HALO, :] = jnp.zeros((HALO, d_b), F32)

    x = x_ref[0]
    hn = _rms(x, ng_ref[...]).astype(BF16)
    q_sc[...] = _dot(hn, w_ref[:, 0 * d_a:1 * d_a])
    z_sc[...] = _dot(hn, w_ref[:, 1 * d_a:2 * d_a])
    v_sc[...] = _dot(hn, w_ref[:, 2 * d_a:3 * d_a]).astype(BF16)
    g_sc[...] = _dot(hn, w_ref[:, 3 * d_a:4 * d_a])
    c0 = 4 * d_a
    u_sc[HALO:HALO + tm, :] = (_dot(hn, w_ref[:, c0:c0 + d_b])
                               * _sigmoid(_dot(hn, w_ref[:, c0 + d_b:c0 + 2 * d_b])))
    gate_sc[...] = _sigmoid(_dot(hn, w_ref[:, c0 + 2 * d_b:c0 + 2 * d_b + 2 * d_m]))

    p = lbp_ref[...]
    pe = jnp.exp(p - jnp.max(p, axis=0, keepdims=True))
    lb = (jnp.sum(pe[:layer + 1], axis=0, keepdims=True)
          / jnp.sum(pe, axis=0, keepdims=True))
    one_m_lb = 1.0 - lb
    hg = hg_ref[...]

    ti = lax.broadcasted_iota(jnp.int32, (CHUNK, CHUNK), 0)
    si = lax.broadcasted_iota(jnp.int32, (CHUNK, CHUNK), 1)
    ltri = (si <= ti).astype(BF16)
    x_ts = ti ^ si
    lev = jnp.where(si > ti, -1, 0)
    for j in range(6):
        lev = jnp.where((si < ti) & (x_ts >= (1 << j)), j + 1, lev)

    pos4 = lax.broadcasted_iota(jnp.int32, (SUBLANES, d_a), 0) & 3
    tri_sub = (lax.broadcasted_iota(jnp.int32, (SUB, SUB), 1)
               <= lax.broadcasted_iota(jnp.int32, (SUB, SUB), 0))

    def intra_any_gate(qs, k, f, b, qe, vb):
        q_lv = [qs.astype(BF16)]
        k_lv = [k.astype(BF16)]
        q_lv.append((qs * f).astype(BF16))
        k_lv.append(k_lv[0])
        e2_parts = []
        for g8 in range(CHUNK // SUBLANES):
            fg = f[g8 * SUBLANES:(g8 + 1) * SUBLANES, :]
            up = pltpu.roll(fg, SUBLANES - 1, axis=0)
            dn = pltpu.roll(fg, 1, axis=0)
            e2_parts.append(jnp.where(pos4 == 0, up,
                                      jnp.where(pos4 == 1, 1.0,
                                                jnp.where(pos4 == 2, fg, fg * dn))))
        e2 = jnp.concatenate(e2_parts, axis=0)
        q_lv.append((qs * e2).astype(BF16))
        k_lv.append((k * e2).astype(BF16))
        for h in reversed(_LEVEL_H):
            parts = []
            for blk in range(CHUNK // (2 * h)):
                row = blk * 2 * h + h - 1
                parts.append(jnp.broadcast_to(b[row:row + 1, :], (2 * h, d_a)))
            r = parts[0] if len(parts) == 1 else jnp.concatenate(parts, axis=0)
            e = jnp.exp2(-jnp.abs(b - r))
            q_lv.append((qs * e).astype(BF16))
            k_lv.append((k * e).astype(BF16))

        outs = []
        for hd in range(heads):
            sl = slice(hd * HEAD_DIM, (hd + 1) * HEAD_DIM)
            scores = jnp.zeros((CHUNK, CHUNK), F32)
            for li in range(len(q_lv)):
                s_l = _dot_nt(q_lv[li][:, sl], k_lv[li][:, sl])
                scores = jnp.where(lev == li, s_l, scores)
            outs.append(_dot(scores.astype(BF16), vb[:, sl]))
        return outs

    def intra_bounded_gate(qs, k, f, b, qe, vb):
        n_sub = CHUNK // SUB
        r_rows = [jnp.zeros((1, d_a), F32)] + [b[j * SUB - 1:j * SUB, :] for j in range(1, n_sub)]
        r_full = jnp.concatenate([jnp.broadcast_to(r, (SUB, d_a)) for r in r_rows], axis=0)
        kt = (k * jnp.exp2(r_full - b)).astype(BF16)
        q_sub = [qe] + [(qs[j * SUB:, :] * jnp.exp2(b[j * SUB:, :] - r_rows[j])).astype(BF16)
                        for j in range(1, n_sub)]
        outs = []
        for hd in range(heads):
            sl = slice(hd * HEAD_DIM, (hd + 1) * HEAD_DIM)
            blocks = [None] * n_sub
            for j in range(n_sub):
                js = slice(j * SUB, (j + 1) * SUB)
                s_j = _dot_nt(q_sub[j][:, sl], kt[js, sl])
                diag = jnp.where(tri_sub, s_j[:SUB, :], 0.0)
                s_j = diag if j == n_sub - 1 else jnp.concatenate([diag, s_j[SUB:, :]], axis=0)
                p_j = _dot(s_j.astype(BF16), vb[js, sl])
                for i in range(j, n_sub):
                    part = p_j[(i - j) * SUB:(i - j + 1) * SUB, :]
                    blocks[i] = part if blocks[i] is None else blocks[i] + part
            outs.append(jnp.concatenate(blocks, axis=0))
        return outs

    def conv_tile(r0):
        for lt in range(d_b // CONV_LANES):
            lanes = slice(lt * CONV_LANES, (lt + 1) * CONV_LANES)
            win = u_sc[pl.ds(r0, CONV_ROWS + HALO), lanes]
            acc = jnp.zeros((CONV_ROWS, CONV_LANES), F32)
            for r in range(SUBLANES):
                win_r = win if r == 0 else pltpu.roll(win, r, axis=0)
                for a in range((width - 1 - r) // SUBLANES + 1):
                    j = SUBLANES * a + r
                    start = HALO - SUBLANES * a
                    acc = acc + (win_r[start:start + CONV_ROWS, :]
                                 * cw_ref[width - 1 - j:width - j, lanes])
            y_sc[pl.ds(r0, CONV_ROWS), lanes] = acc

    def make_chunk(intra):
        def chunk(c, carry):
            rows = pl.ds(pl.multiple_of(c * CHUNK, CHUNK), CHUNK)
            qs = _silu(q_sc[rows, :])
            k = one_m_lb * (1.0 - _sigmoid(z_sc[rows, :]))
            f = 1.0 - k
            vb = v_sc[rows, :]
            gate = _silu(g_sc[rows, :])
            lf = jnp.log(f) * LOG2E
            lf_hi = lf.astype(BF16)
            lf_lo = (lf - lf_hi.astype(F32)).astype(BF16)
            b = _dot(ltri, lf_hi) + _dot(ltri, lf_lo)
            b_last = b[CHUNK - 1:CHUNK, :]
            qe = (qs * jnp.exp2(b)).astype(BF16)
            kd = (k * jnp.exp2(b_last - b)).astype(BF16)
            dec = jnp.exp2(b_last)
            o_intra = intra(qs, k, f, b, qe, vb)
            for hd in range(heads):
                sl = slice(hd * HEAD_DIM, (hd + 1) * HEAD_DIM)
                st = st_ref[hd]
                o = _dot_nt(qe[:, sl], st.astype(BF16)) + o_intra[hd]
                ms = jnp.mean(o * o, axis=-1, keepdims=True)
                on = o * lax.rsqrt(ms + EPS) * hg
                oa_sc[rows, sl] = (on * gate[:, sl]).astype(BF16)
                st_ref[hd] = st * dec[:, sl] + _dot_tn(vb[:, sl], kd[:, sl])
            conv_tile(pl.multiple_of(c * CONV_ROWS, CONV_ROWS))
            return carry
        return chunk

    gates_bounded = jnp.min(lb) >= SUB_MIN_GATE

    @pl.when(gates_bounded)
    def _():
        lax.fori_loop(0, tm // CHUNK, make_chunk(intra_bounded_gate), 0)

    @pl.when(jnp.logical_not(gates_bounded))
    def _():
        lax.fori_loop(0, tm // CHUNK, make_chunk(intra_any_gate), 0)

    u_sc[0:HALO, :] = u_sc[tm:tm + HALO, :]

    branch_a = _dot(oa_sc[...], wa_ref[...])
    y = y_sc[...] + cb_ref[...]
    mu = jnp.mean(y, axis=-1, keepdims=True)
    yc = y - mu
    var = jnp.mean(yc * yc, axis=-1, keepdims=True)
    ln = yc * lax.rsqrt(var + EPS) * lng_ref[...] + lnb_ref[...]
    branch_b = _dot(_silu(ln).astype(BF16), wb_ref[...])
    merged = gate_sc[:, :d_m] * branch_a + gate_sc[:, d_m:] * branch_b
    o_ref[0] = x + _dot(merged.astype(BF16), wo_ref[...])


def _mlp_kernel(x_ref, ng_ref, w1_ref, w2_ref, fg_ref, o_ref):
    x = x_ref[...]
    hn = _rms(x, ng_ref[...]).astype(BF16)
    acc = x
    for j in range(w1_ref.shape[1] // FF_TILE):
        cols = slice(j * FF_TILE, (j + 1) * FF_TILE)
        z = jnp.maximum(_dot(hn, w1_ref[:, cols]), 0.0)
        acc = acc + _dot((z * z).astype(BF16), w2_ref[cols, :])
    o_ref[...] = _rms(acc, fg_ref[...])


def _const_spec(shape):
    return pl.BlockSpec(shape, lambda *_: (0,) * len(shape), pipeline_mode=pl.Buffered(1))


def _layer(x, l, norm_mix_g, w_in, lb_param, hgrn_norm_g, w_a_out, conv_w, conv_b, conv_ln_g,
           conv_ln_b, w_b_out, w_out, norm_mlp_g, w_mlp_in, w_mlp_out, out_g):
    bsz, seq, d = x.shape
    d_a = w_a_out.shape[1]
    d_b = w_b_out.shape[1]
    heads = d_a // HEAD_DIM
    width = conv_w.shape[1]
    d_ff = w_mlp_in.shape[2]
    assert seq % TM_MIX == 0 and TM_MIX % CHUNK == 0 and (bsz * seq) % TM_MLP == 0
    assert width - 1 <= HALO <= TM_MIX and d_ff % FF_TILE == 0
    assert TM_MIX % CONV_ROWS == 0 and d_b % CONV_LANES == 0
    assert w_in.shape[2] == 4 * d_a + 2 * d_b + 2 * d

    row = lambda v: v.reshape(1, -1)
    tok_spec = pl.BlockSpec((1, TM_MIX, d), lambda b, s: (b, s, 0))
    x1 = pl.pallas_call(
        functools.partial(_mixer_kernel, layer=l, heads=heads, width=width),
        grid=(bsz, seq // TM_MIX),
        in_specs=[tok_spec, _const_spec((1, d)), _const_spec(w_in.shape[1:]),
                  _const_spec(lb_param.shape), _const_spec((1, HEAD_DIM)), _const_spec((d_a, d)),
                  _const_spec((width, d_b)), _const_spec((1, d_b)), _const_spec((1, d_b)),
                  _const_spec((1, d_b)), _const_spec((d_b, d)), _const_spec((d, d))],
        out_specs=tok_spec,
        out_shape=jax.ShapeDtypeStruct((bsz, seq, d), F32),
        scratch_shapes=[pltpu.VMEM((heads, HEAD_DIM, HEAD_DIM), F32),
                        pltpu.VMEM((TM_MIX, d_a), F32),
                        pltpu.VMEM((TM_MIX, d_a), F32),
                        pltpu.VMEM((TM_MIX, d_a), BF16),
                        pltpu.VMEM((TM_MIX, d_a), F32),
                        pltpu.VMEM((TM_MIX, d_a), BF16),
                        pltpu.VMEM((HALO + TM_MIX, d_b), F32),
                        pltpu.VMEM((TM_MIX, d_b), F32),
                        pltpu.VMEM((TM_MIX, 2 * d), F32)],
        compiler_params=pltpu.CompilerParams(
            dimension_semantics=("arbitrary", "arbitrary"), vmem_limit_bytes=VMEM_LIMIT),
        name="mixer",
    )(x, row(norm_mix_g[l]), w_in[l].astype(BF16), lb_param, row(hgrn_norm_g[l]),
      w_a_out[l].astype(BF16), conv_w[l], row(conv_b[l]), row(conv_ln_g[l]), row(conv_ln_b[l]),
      w_b_out[l].astype(BF16), w_out[l].astype(BF16))

    n_tok = bsz * seq
    flat_spec = pl.BlockSpec((TM_MLP, d), lambda i: (i, 0))
    out = pl.pallas_call(
        _mlp_kernel,
        grid=(n_tok // TM_MLP,),
        in_specs=[flat_spec, _const_spec((1, d)), _const_spec((d, d_ff)),
                  _const_spec((d_ff, d)), _const_spec((1, d))],
        out_specs=flat_spec,
        out_shape=jax.ShapeDtypeStruct((n_tok, d), F32),
        compiler_params=pltpu.CompilerParams(
            dimension_semantics=("arbitrary",), vmem_limit_bytes=VMEM_LIMIT),
        name="mlp",
    )(x1.reshape(n_tok, d), row(norm_mlp_g[l]), w_mlp_in[l].astype(BF16),
      w_mlp_out[l].astype(BF16), row(out_g))
    return out.reshape(bsz, seq, d)


def kernel(x, norm_mix_g, w_in, lb_param, hgrn_norm_g, w_a_out, conv_w, conv_b, conv_ln_g,
           conv_ln_b, w_b_out, w_out, norm_mlp_g, w_mlp_in, w_mlp_out, norm_final_g):
    depth = w_in.shape[0]
    assert depth == 1, "final norm is fused into the (single) layer's MLP kernel"
    return _layer(x, 0, norm_mix_g, w_in, lb_param, hgrn_norm_g, w_a_out, conv_w, conv_b,
                  conv_ln_g, conv_ln_b, w_b_out, w_out, norm_mlp_g, w_mlp_in, w_mlp_out,
                  norm_final_g)
```
